```python
import jax, jax.numpy as jnp
from jax import lax
import numpy as np

D_MODEL = 1024
BATCH = 16
SEQ = 4096
DEPTH = 1
DEC_BATCH = 2
DEC_SEQ = 8192
PAST_LEN = 128

EPS = 1e-6
POOL_WINDOWS = (2, 4, 8, 16)
N_POOL_GROUPS = len(POOL_WINDOWS)
POOL_GROUP_DIM = 128
D_POOL = N_POOL_GROUPS * POOL_GROUP_DIM
POOL_OUT_DIM = D_MODEL // N_POOL_GROUPS
CHUNK = 128
N_SGU_HEADS = 8
SGU_HEAD_DIM = 64
D_SGU = N_SGU_HEADS * SGU_HEAD_DIM
N_BRANCH = 2
D_IN = D_POOL + 2 * D_SGU + N_BRANCH * D_MODEL
D_FF = 2816
CONV_WIDTH = 3
PLE_DIM = 256

kernel_name = "hybrid_pool_sgu_encoder"


def rms_norm(x, g):
    xf = x.astype(jnp.float32)
    y = xf * lax.rsqrt(jnp.mean(xf * xf, axis=-1, keepdims=True) + EPS)
    return (y * g.astype(jnp.float32)).astype(x.dtype)


def pool_mixer(z, pool_w, pool_scale):
    B, S, _ = z.shape
    zg = z.reshape(B, S, N_POOL_GROUPS, POOL_GROUP_DIM)
    csum = jnp.cumsum(zg.astype(jnp.float32), axis=1)
    csum = jnp.concatenate([jnp.zeros_like(csum[:, :1]), csum], axis=1)
    t = jnp.arange(S, dtype=jnp.int32)
    outs = []
    for gi, w in enumerate(POOL_WINDOWS):
        lo = jnp.clip(t - w // 2, 0, S)
        hi = jnp.clip(t + w // 2, 0, S)
        cg = csum[:, :, gi]
        s = jnp.take(cg, hi, axis=1) - jnp.take(cg, lo, axis=1)
        mean = s / (hi - lo).astype(jnp.float32)[None, :, None]
        outs.append(mean - zg[:, :, gi].astype(jnp.float32))
    d = jnp.stack(outs, axis=2).astype(z.dtype)
    y = jnp.einsum('bsgc,gcd->bsgd', d, pool_w).reshape(B, S, D_MODEL)
    return y * pool_scale


def sgu_mixer(z_uv, g_v, sgu_w, sgu_b, w_gproj):
    B, S, _ = z_uv.shape
    uv = jax.nn.gelu(z_uv, approximate=False)
    u, v = jnp.split(uv, 2, axis=-1)
    v = v.reshape(B, S // CHUNK, CHUNK, N_SGU_HEADS, SGU_HEAD_DIM)
    v = rms_norm(v, g_v)
    mixed = jnp.einsum('hts,bnshc->bnthc', sgu_w, v) + sgu_b.T[:, :, None]
    gated = u * mixed.reshape(B, S, D_SGU)
    return gated @ w_gproj


def depthwise_conv_centred(a, w, b):
    ap = jnp.pad(a, ((0, 0), (1, 1), (0, 0)))
    return ap[:, :-2] * w[0] + ap[:, 1:-1] * w[1] + ap[:, 2:] * w[2] + b


def layer(x, p, g_mix, w_in, pool_w, pool_scale, g_v, sgu_w, sgu_b, w_gproj, w_o,
          g_ffn, w_up, conv_w, conv_b, w_down, g_pe, w_pe, w_pg):
    B, S, _ = x.shape
    h = rms_norm(x, g_mix)
    proj = h @ w_in
    z_pool, z_uv, gate_logits = jnp.split(proj, [D_POOL, D_POOL + 2 * D_SGU], axis=-1)
    gates = jax.nn.sigmoid(gate_logits.reshape(B, S, N_BRANCH, D_MODEL))
    a = pool_mixer(z_pool, pool_w, pool_scale)
    c = sgu_mixer(z_uv, g_v, sgu_w, sgu_b, w_gproj)
    merged = gates[:, :, 0] * a + gates[:, :, 1] * c
    x = x + merged @ w_o
    h2 = rms_norm(x, g_ffn)
    up = depthwise_conv_centred(h2 @ w_up, conv_w, conv_b)
    gate, val = jnp.split(up, 2, axis=-1)
    x = x + (jax.nn.gelu(gate, approximate=False) * val) @ w_down
    pe_gate = jax.nn.sigmoid(rms_norm(x, g_pe) @ w_pg)
    x = x + (p.astype(x.dtype) @ w_pe) * pe_gate
    return x


def encoder(x, p, g_mix, w_in, pool_w, pool_scale, g_v, sgu_w, sgu_b, w_gproj, w_o,
            g_ffn, w_up, conv_w, conv_b, w_down, g_pe, w_pe, w_pg, g_final):
    for i in range(DEPTH):
        x = layer(x, p[i], g_mix[i], w_in[i], pool_w[i], pool_scale[i], g_v[i], sgu_w[i],
                  sgu_b[i], w_gproj[i], w_o[i], g_ffn[i], w_up[i], conv_w[i], conv_b[i],
                  w_down[i], g_pe[i], w_pe[i], w_pg[i])
    return rms_norm(x, g_final)


def setup_inputs(seed: int = 0) -> dict:
    key = jax.random.key(seed)
    ks = jax.random.split(key, 24)
    f32 = jnp.float32
    L = DEPTH

    def nrm(k, shape, scale):
        return jax.random.normal(k, shape, f32) * scale

    def gain(k, shape):
        return 1.0 + 0.02 * jax.random.normal(k, shape, f32)

    return {
        "x_prompt": nrm(ks[0], (BATCH, SEQ, D_MODEL), 1.0),
        "x_sample": nrm(ks[1], (DEC_BATCH, DEC_SEQ, D_MODEL), 1.0),
        "p_prompt": nrm(ks[2], (DEPTH, BATCH, SEQ, PLE_DIM), 1.0),
        "p_sample": nrm(ks[3], (DEPTH, DEC_BATCH, DEC_SEQ, PLE_DIM), 1.0),
        "g_mix": gain(ks[4], (L, D_MODEL)),
        "w_in": nrm(ks[5], (L, D_MODEL, D_IN), D_MODEL ** -0.5),
        "pool_w": nrm(ks[6], (L, N_POOL_GROUPS, POOL_GROUP_DIM, POOL_OUT_DIM), POOL_GROUP_DIM ** -0.5),
        "pool_scale": gain(ks[7], (L, D_MODEL)),
        "g_v": gain(ks[8], (L, N_SGU_HEADS, SGU_HEAD_DIM)),
        "sgu_w": nrm(ks[9], (L, N_SGU_HEADS, CHUNK, CHUNK), CHUNK ** -0.5),
        "sgu_b": 1.0 + 0.01 * jax.random.normal(ks[10], (L, N_SGU_HEADS, CHUNK), f32),
        "w_gproj": nrm(ks[11], (L, D_SGU, D_MODEL), D_SGU ** -0.5),
        "w_o": nrm(ks[12], (L, D_MODEL, D_MODEL), D_MODEL ** -0.5),
        "g_ffn": gain(ks[13], (L, D_MODEL)),
        "w_up": nrm(ks[14], (L, D_MODEL, 2 * D_FF), D_MODEL ** -0.5),
        "conv_w": nrm(ks[15], (L, CONV_WIDTH, 2 * D_FF), CONV_WIDTH ** -0.5),
        "conv_b": nrm(ks[16], (L, 2 * D_FF), 0.01),
        "w_down": nrm(ks[17], (L, D_FF, D_MODEL), D_FF ** -0.5),
        "g_pe": gain(ks[18], (L, D_MODEL)),
        "w_pe": nrm(ks[19], (L, PLE_DIM, D_MODEL), PLE_DIM ** -0.5),
        "w_pg": nrm(ks[20], (L, D_MODEL, D_MODEL), D_MODEL ** -0.5),
        "g_final": gain(ks[21], (D_MODEL,)),
    }


def reference(x_prompt, x_sample, p_prompt, p_sample, g_mix, w_in, pool_w, pool_scale, g_v,
              sgu_w, sgu_b, w_gproj, w_o, g_ffn, w_up, conv_w, conv_b, w_down, g_pe, w_pe,
              w_pg, g_final):
    y_prompt = encoder(x_prompt, p_prompt, g_mix, w_in, pool_w, pool_scale, g_v, sgu_w, sgu_b,
                       w_gproj, w_o, g_ffn, w_up, conv_w, conv_b, w_down, g_pe, w_pe, w_pg, g_final)
    y_sample = encoder(x_sample, p_sample, g_mix, w_in, pool_w, pool_scale, g_v, sgu_w, sgu_b,
                       w_gproj, w_o, g_ffn, w_up, conv_w, conv_b, w_down, g_pe, w_pe, w_pg, g_final)
    return (y_prompt, y_sample)
```

```python
import functools
import math

import jax
import jax.numpy as jnp
from jax import lax
from jax.experimental import pallas as pl
from jax.experimental.pallas import tpu as pltpu

D_MODEL = 1024
EPS = 1e-6
POOL_WINDOWS = (2, 4, 8, 16)
N_POOL_GROUPS = len(POOL_WINDOWS)
POOL_GROUP_DIM = 128
D_POOL = N_POOL_GROUPS * POOL_GROUP_DIM
POOL_OUT_DIM = D_MODEL // N_POOL_GROUPS
CHUNK = 128
N_SGU_HEADS = 8
SGU_HEAD_DIM = 64
D_SGU = N_SGU_HEADS * SGU_HEAD_DIM
D_FF = 2816
PLE_DIM = 256

LANES = 128
HALO = 8
SEQ_TILE = 256
VMEM_LIMIT_BYTES = 56 * 1024 * 1024
SQRT_HALF = math.sqrt(0.5)

assert max(POOL_WINDOWS) // 2 <= HALO
assert SEQ_TILE % CHUNK == 0


def _rms(x, g):
    return x * lax.rsqrt(jnp.mean(x * x, axis=-1, keepdims=True) + EPS) * g


def _gelu(x):
    return 0.5 * x * (1.0 + lax.erf(x * SQRT_HALF))


def _bdot(a, b):
    return jnp.dot(a, b, preferred_element_type=jnp.float32)


def _halo_rows(prev_ref, next_ref):
    s = pl.program_id(1)
    prev = jnp.where(s > 0, prev_ref[0], 0.0)
    nxt = jnp.where(s < pl.num_programs(1) - 1, next_ref[0], 0.0)
    return jnp.concatenate([prev, nxt], axis=0)


def _store_with_halo(dst_ref, j, vals, ts):
    dst_ref[j, HALO:HALO + ts, :] = vals[:ts]
    dst_ref[j, 0:HALO, :] = vals[ts:ts + HALO]
    dst_ref[j, HALO + ts:, :] = vals[ts + HALO:]


def _mixer_kernel(xp_ref, xc_ref, xn_ref, gmix_ref, wpool_ref, wrest_ref,
                  poolw_ref, pscale_ref, gv_ref, msmat_ref, sguw_ref, sgub_ref,
                  wg_ref, wo_ref, o_ref, zp_ref, *, seq_len):
    ts = xc_ref.shape[1]
    xc = xc_ref[0]
    gmix = gmix_ref[...]
    hc = _rms(xc, gmix).astype(jnp.bfloat16)
    hh = _rms(_halo_rows(xp_ref, xn_ref), gmix).astype(jnp.bfloat16)

    zp = _bdot(jnp.concatenate([hc, hh], axis=0), wpool_ref[...])
    for g in range(N_POOL_GROUPS):
        _store_with_halo(zp_ref, g, zp[:, g * LANES:(g + 1) * LANES], ts)

    t = pl.program_id(1) * ts + lax.broadcasted_iota(jnp.int32, (ts, 1), 0)
    pooled = []
    for g, w in enumerate(POOL_WINDOWS):
        acc = zp_ref[g, HALO - w // 2:HALO - w // 2 + ts, :]
        for k in range(1 - w // 2, w // 2):
            acc = acc + zp_ref[g, HALO + k:HALO + k + ts, :]
        cnt = jnp.minimum(t + w // 2, seq_len) - jnp.maximum(t - w // 2, 0)
        d = acc / cnt.astype(jnp.float32) - zp_ref[g, HALO:HALO + ts, :]
        pooled.append(_bdot(d.astype(jnp.bfloat16), poolw_ref[g]))
    a = jnp.concatenate(pooled, axis=-1) * pscale_ref[...]

    proj = _bdot(hc, wrest_ref[...])

    uv = _gelu(proj[:, :2 * D_SGU])
    u = uv[:, :D_SGU]
    v = uv[:, D_SGU:]
    ms = _bdot((v * v).astype(jnp.bfloat16), msmat_ref[...])
    vn = (v * lax.rsqrt(ms + EPS) * gv_ref[...]).astype(jnp.bfloat16)
    first_head = lax.broadcasted_iota(jnp.int32, (CHUNK, LANES), 1) < SGU_HEAD_DIM
    sgub = sgub_ref[...]
    gated = []
    for n in range(ts // CHUNK):
        rows = slice(n * CHUNK, (n + 1) * CHUNK)
        mixed = []
        for j in range(N_SGU_HEADS // 2):
            r = _bdot(sguw_ref[j], vn[rows, j * LANES:(j + 1) * LANES])
            mixed.append(jnp.where(first_head, r[:CHUNK], r[CHUNK:]))
        mixed = jnp.concatenate(mixed, axis=-1) + sgub
        gated.append((u[rows] * mixed).astype(jnp.bfloat16))
    c = _bdot(jnp.concatenate(gated, axis=0), wg_ref[...])

    gates = jax.nn.sigmoid(proj[:, 2 * D_SGU:])
    merged = gates[:, :D_MODEL] * a + gates[:, D_MODEL:] * c
    o_ref[0] = xc + _bdot(merged.astype(jnp.bfloat16), wo_ref[...])


def _ffn_kernel(xp_ref, xc_ref, xn_ref, p_ref, gffn_ref, wup_ref, convw_ref,
                convb_ref, wdown_ref, gpe_ref, wpg_ref, wpe_ref, gfin_ref,
                o_ref, up_ref):
    ts = xc_ref.shape[1]
    xc = xc_ref[0]
    gffn = gffn_ref[...]
    hc = _rms(xc, gffn).astype(jnp.bfloat16)
    hh = _rms(_halo_rows(xp_ref, xn_ref), gffn).astype(jnp.bfloat16)
    up = _bdot(jnp.concatenate([hc, hh], axis=0), wup_ref[...])
    n_slabs = 2 * D_FF // LANES
    for j in range(n_slabs):
        _store_with_halo(up_ref, j, up[:, j * LANES:(j + 1) * LANES], ts)

    def conv(j):
        cols = slice(j * LANES, (j + 1) * LANES)
        return (up_ref[j, HALO - 1:HALO - 1 + ts, :] * convw_ref[0:1, cols]
                + up_ref[j, HALO:HALO + ts, :] * convw_ref[1:2, cols]
                + up_ref[j, HALO + 1:HALO + 1 + ts, :] * convw_ref[2:3, cols]
                + convb_ref[:, cols])

    half = n_slabs // 2
    act = [(_gelu(conv(j)) * conv(half + j)).astype(jnp.bfloat16)
           for j in range(half)]
    x2 = xc + _bdot(jnp.concatenate(act, axis=-1), wdown_ref[...])

    h3 = _rms(x2, gpe_ref[...]).astype(jnp.bfloat16)
    pe_gate = jax.nn.sigmoid(_bdot(h3, wpg_ref[...]))
    pe = _bdot(p_ref[0].astype(jnp.bfloat16), wpe_ref[...])
    o_ref[0] = _rms(x2 + pe * pe_gate, gfin_ref[...])


def _const_spec(shape):
    return pl.BlockSpec(shape, lambda b, s: (0,) * len(shape),
                        pipeline_mode=pl.Buffered(1))


def _token_specs(ts, seq_len, width):
    blocks_per_tile = ts // HALO
    last_block = seq_len // HALO - 1
    prev = pl.BlockSpec(
        (1, HALO, width),
        lambda b, s: (b, jnp.maximum(s * blocks_per_tile - 1, 0), 0))
    cur = pl.BlockSpec((1, ts, width), lambda b, s: (b, s, 0))
    nxt = pl.BlockSpec(
        (1, HALO, width),
        lambda b, s: (b, jnp.minimum((s + 1) * blocks_per_tile, last_block), 0))
    return prev, cur, nxt


def _compiler_params():
    return pltpu.CompilerParams(
        dimension_semantics=("parallel", "parallel"),
        vmem_limit_bytes=VMEM_LIMIT_BYTES)


def _mixer_call(x, weights):
    batch, seq_len, _ = x.shape
    ts = SEQ_TILE
    assert seq_len % ts == 0
    prev, cur, nxt = _token_specs(ts, seq_len, D_MODEL)
    return pl.pallas_call(
        functools.partial(_mixer_kernel, seq_len=seq_len),
        grid=(batch, seq_len // ts),
        in_specs=[prev, cur, nxt] + [_const_spec(w.shape) for w in weights],
        out_specs=cur,
        out_shape=jax.ShapeDtypeStruct(x.shape, x.dtype),
        scratch_shapes=[
            pltpu.VMEM((N_POOL_GROUPS, ts + 2 * HALO, LANES), jnp.float32)],
        compiler_params=_compiler_params(),
        name="mixer",
    )(x, x, x, *weights)


def _ffn_call(x, p, weights):
    batch, seq_len, _ = x.shape
    ts = SEQ_TILE
    assert seq_len % ts == 0
    prev, cur, nxt = _token_specs(ts, seq_len, D_MODEL)
    p_spec = pl.BlockSpec((1, ts, PLE_DIM), lambda b, s: (b, s, 0))
    return pl.pallas_call(
        _ffn_kernel,
        grid=(batch, seq_len // ts),
        in_specs=[prev, cur, nxt, p_spec] + [_const_spec(w.shape) for w in weights],
        out_specs=cur,
        out_shape=jax.ShapeDtypeStruct(x.shape, x.dtype),
        scratch_shapes=[
            pltpu.VMEM((2 * D_FF // LANES, ts + 2 * HALO, LANES), jnp.float32)],
        compiler_params=_compiler_params(),
        name="ffn",
    )(x, x, x, p, *weights)


def kernel(x_prompt, x_sample, p_prompt, p_sample, g_mix, w_in, pool_w, pool_scale, g_v, sgu_w, sgu_b, w_gproj, w_o, g_ffn, w_up, conv_w, conv_b, w_down, g_pe, w_pe, w_pg, g_final):
    bf16 = jnp.bfloat16
    depth = w_in.shape[0]
    assert depth == 1

    def row(v):
        return v.reshape(1, -1)

    head_of = jnp.arange(D_SGU) // SGU_HEAD_DIM
    msmat = jnp.where(head_of[:, None] == head_of[None, :],
                      1.0 / SGU_HEAD_DIM, 0.0).astype(bf16)

    xs = [x_prompt, x_sample]
    ps = [p_prompt, p_sample]
    for i in range(depth):
        sguw = sgu_w[i].astype(bf16).reshape(N_SGU_HEADS // 2, 2 * CHUNK, CHUNK)
        sgub = jnp.repeat(sgu_b[i].T, SGU_HEAD_DIM, axis=1)
        mixer_w = (row(g_mix[i]), w_in[i, :, :D_POOL].astype(bf16),
                   w_in[i, :, D_POOL:].astype(bf16), pool_w[i].astype(bf16),
                   row(pool_scale[i]), row(g_v[i]), msmat, sguw, sgub,
                   w_gproj[i].astype(bf16), w_o[i].astype(bf16))
        ffn_w = (row(g_ffn[i]), w_up[i].astype(bf16), conv_w[i], row(conv_b[i]),
                 w_down[i].astype(bf16), row(g_pe[i]), w_pg[i].astype(bf16),
                 w_pe[i].astype(bf16), row(g_final))
        xs = [_ffn_call(_mixer_call(x, mixer_w), p[i], ffn_w)
              for x, p in zip(xs, ps)]
    return tuple(xs)
```

```python
import functools
import math

import jax
import jax.numpy as jnp
from jax import lax
from jax.experimental import pallas as pl
from jax.experimental.pallas import tpu as pltpu

D_MODEL = 1024
EPS = 1e-6
POOL_WINDOWS = (2, 4, 8, 16)
N_POOL_GROUPS = len(POOL_WINDOWS)
POOL_GROUP_DIM = 128
D_POOL = N_POOL_GROUPS * POOL_GROUP_DIM
POOL_OUT_DIM = D_MODEL // N_POOL_GROUPS
CHUNK = 128
N_SGU_HEADS = 8
SGU_HEAD_DIM = 64
D_SGU = N_SGU_HEADS * SGU_HEAD_DIM
D_FF = 2816
PLE_DIM = 256

LANES = 128
HALO = 8
SEQ_TILE = 512
MIXER_SUB_TILE = 256
FFN_SUB_TILE = 512
VMEM_LIMIT_BYTES = 56 * 1024 * 1024
SQRT_HALF = math.sqrt(0.5)

assert max(POOL_WINDOWS) // 2 <= HALO
assert MIXER_SUB_TILE % CHUNK == 0 and SEQ_TILE % MIXER_SUB_TILE == 0
assert SEQ_TILE % FFN_SUB_TILE == 0


def _rms(x, g):
    return x * lax.rsqrt(jnp.mean(x * x, axis=-1, keepdims=True) + EPS) * g


def _gelu(x):
    return 0.5 * x * (1.0 + lax.erf(x * SQRT_HALF))


def _bdot(a, b):
    return jnp.dot(a, b, preferred_element_type=jnp.float32)


def _run_interleaved(chains):
    chains = list(chains)
    while chains:
        chains = [c for c in chains if next(c, StopIteration) is not StopIteration]


def _halo_rows(xp_ref, xc_ref, xn_ref, r0, sub):
    s = pl.program_id(1)
    if r0 == 0:
        prev = jnp.where(s > 0, xp_ref[0], 0.0)
    else:
        prev = xc_ref[0, r0 - HALO:r0, :]
    if r0 + sub == xc_ref.shape[1]:
        nxt = jnp.where(s < pl.num_programs(1) - 1, xn_ref[0], 0.0)
    else:
        nxt = xc_ref[0, r0 + sub:r0 + sub + HALO, :]
    return jnp.concatenate([prev, nxt], axis=0)


def _store_with_halo(dst_ref, vals, sub):
    dst_ref[HALO:HALO + sub, :] = vals[:sub]
    dst_ref[0:HALO, :] = vals[sub:sub + HALO]
    dst_ref[HALO + sub:, :] = vals[sub + HALO:]


def _mixer_sub_tile(xc, halo, t, o_ref, zp_ref, gmix_ref, wpool_ref, wrest_ref,
                    poolw_ref, pscale_ref, gv_ref, msmat_ref, sguw_ref,
                    sgub_ref, wg_ref, wo_ref, *, seq_len):
    sub = xc.shape[0]
    gmix = gmix_ref[...]
    hc = _rms(xc, gmix).astype(jnp.bfloat16)
    hh = _rms(halo, gmix).astype(jnp.bfloat16)

    zp = _bdot(jnp.concatenate([hc, hh], axis=0), wpool_ref[...])
    for g in range(N_POOL_GROUPS):
        _store_with_halo(zp_ref.at[g], zp[:, g * LANES:(g + 1) * LANES], sub)
    yield

    proj = _bdot(hc, wrest_ref[...])
    yield

    pooled = []
    for g, w in enumerate(POOL_WINDOWS):
        acc = zp_ref[g, HALO - w // 2:HALO - w // 2 + sub, :]
        for k in range(1 - w // 2, w // 2):
            acc = acc + zp_ref[g, HALO + k:HALO + k + sub, :]
        cnt = jnp.minimum(t + w // 2, seq_len) - jnp.maximum(t - w // 2, 0)
        d = acc / cnt.astype(jnp.float32) - zp_ref[g, HALO:HALO + sub, :]
        pooled.append(_bdot(d.astype(jnp.bfloat16), poolw_ref[g]))
    a = jnp.concatenate(pooled, axis=-1) * pscale_ref[...]
    yield

    uv = _gelu(proj[:, :2 * D_SGU])
    u = uv[:, :D_SGU]
    v = uv[:, D_SGU:]
    ms = _bdot((v * v).astype(jnp.bfloat16), msmat_ref[...])
    yield
    vn = (v * lax.rsqrt(ms + EPS) * gv_ref[...]).astype(jnp.bfloat16)
    first_head = lax.broadcasted_iota(jnp.int32, (CHUNK, LANES), 1) < SGU_HEAD_DIM
    sgub = sgub_ref[...]
    gated = []
    for n in range(sub // CHUNK):
        rows = slice(n * CHUNK, (n + 1) * CHUNK)
        mixed = []
        for j in range(N_SGU_HEADS // 2):
            r = _bdot(sguw_ref[j], vn[rows, j * LANES:(j + 1) * LANES])
            mixed.append(jnp.where(first_head, r[:CHUNK], r[CHUNK:]))
        mixed = jnp.concatenate(mixed, axis=-1) + sgub
        gated.append((u[rows] * mixed).astype(jnp.bfloat16))
    yield
    c = _bdot(jnp.concatenate(gated, axis=0), wg_ref[...])
    yield

    gates = jax.nn.sigmoid(proj[:, 2 * D_SGU:])
    merged = gates[:, :D_MODEL] * a + gates[:, D_MODEL:] * c
    o_ref[...] = xc + _bdot(merged.astype(jnp.bfloat16), wo_ref[...])


def _mixer_kernel(xp_ref, xc_ref, xn_ref, *rest, seq_len, sub):
    *w_refs, o_ref, zp_ref = rest
    ts = xc_ref.shape[1]
    chains = []
    for i in range(ts // sub):
        r0 = i * sub
        t = (pl.program_id(1) * ts + r0
             + lax.broadcasted_iota(jnp.int32, (sub, 1), 0))
        chains.append(_mixer_sub_tile(
            xc_ref[0, r0:r0 + sub, :], _halo_rows(xp_ref, xc_ref, xn_ref, r0, sub),
            t, o_ref.at[0, r0:r0 + sub, :], zp_ref.at[i], *w_refs,
            seq_len=seq_len))
    _run_interleaved(chains)


def _ffn_sub_tile(xc, halo, p, o_ref, up_ref, gffn_ref, wup_ref, convw_ref,
                  convb_ref, wdown_ref, gpe_ref, wpg_ref, wpe_ref, gfin_ref):
    sub = xc.shape[0]
    gffn = gffn_ref[...]
    hc = _rms(xc, gffn).astype(jnp.bfloat16)
    hh = _rms(halo, gffn).astype(jnp.bfloat16)
    up = _bdot(jnp.concatenate([hc, hh], axis=0), wup_ref[...])
    n_slabs = 2 * D_FF // LANES
    for j in range(n_slabs):
        _store_with_halo(up_ref.at[j], up[:, j * LANES:(j + 1) * LANES], sub)
    yield

    def conv(j):
        cols = slice(j * LANES, (j + 1) * LANES)
        return (up_ref[j, HALO - 1:HALO - 1 + sub, :] * convw_ref[0:1, cols]
                + up_ref[j, HALO:HALO + sub, :] * convw_ref[1:2, cols]
                + up_ref[j, HALO + 1:HALO + 1 + sub, :] * convw_ref[2:3, cols]
                + convb_ref[:, cols])

    half = n_slabs // 2
    act = [(_gelu(conv(j)) * conv(half + j)).astype(jnp.bfloat16)
           for j in range(half)]
    x2 = xc + _bdot(jnp.concatenate(act, axis=-1), wdown_ref[...])
    yield

    h3 = _rms(x2, gpe_ref[...]).astype(jnp.bfloat16)
    pe_gate = jax.nn.sigmoid(_bdot(h3, wpg_ref[...]))
    pe = _bdot(p.astype(jnp.bfloat16), wpe_ref[...])
    yield
    o_ref[...] = _rms(x2 + pe * pe_gate, gfin_ref[...])


def _ffn_kernel(xp_ref, xc_ref, xn_ref, p_ref, *rest, sub):
    *w_refs, o_ref, up_ref = rest
    ts = xc_ref.shape[1]
    chains = []
    for i in range(ts // sub):
        r0 = i * sub
        chains.append(_ffn_sub_tile(
            xc_ref[0, r0:r0 + sub, :], _halo_rows(xp_ref, xc_ref, xn_ref, r0, sub),
            p_ref[0, r0:r0 + sub, :], o_ref.at[0, r0:r0 + sub, :], up_ref.at[i],
            *w_refs))
    _run_interleaved(chains)


def _const_spec(shape):
    return pl.BlockSpec(shape, lambda b, s: (0,) * len(shape),
                        pipeline_mode=pl.Buffered(1))


def _token_specs(ts, seq_len, width):
    blocks_per_tile = ts // HALO
    last_block = seq_len // HALO - 1
    prev = pl.BlockSpec(
        (1, HALO, width),
        lambda b, s: (b, jnp.maximum(s * blocks_per_tile - 1, 0), 0))
    cur = pl.BlockSpec((1, ts, width), lambda b, s: (b, s, 0))
    nxt = pl.BlockSpec(
        (1, HALO, width),
        lambda b, s: (b, jnp.minimum((s + 1) * blocks_per_tile, last_block), 0))
    return prev, cur, nxt


def _compiler_params():
    return pltpu.CompilerParams(
        dimension_semantics=("parallel", "parallel"),
        vmem_limit_bytes=VMEM_LIMIT_BYTES)


def _mixer_call(x, weights):
    batch, seq_len, _ = x.shape
    ts, sub = SEQ_TILE, MIXER_SUB_TILE
    assert seq_len % ts == 0
    prev, cur, nxt = _token_specs(ts, seq_len, D_MODEL)
    return pl.pallas_call(
        functools.partial(_mixer_kernel, seq_len=seq_len, sub=sub),
        grid=(batch, seq_len // ts),
        in_specs=[prev, cur, nxt] + [_const_spec(w.shape) for w in weights],
        out_specs=cur,
        out_shape=jax.ShapeDtypeStruct(x.shape, x.dtype),
        scratch_shapes=[pltpu.VMEM(
            (ts // sub, N_POOL_GROUPS, sub + 2 * HALO, LANES), jnp.float32)],
        compiler_params=_compiler_params(),
        name="mixer",
    )(x, x, x, *weights)


def _ffn_call(x, p, weights):
    batch, seq_len, _ = x.shape
    ts, sub = SEQ_TILE, FFN_SUB_TILE
    assert seq_len % ts == 0
    prev, cur, nxt = _token_specs(ts, seq_len, D_MODEL)
    p_spec = pl.BlockSpec((1, ts, PLE_DIM), lambda b, s: (b, s, 0))
    return pl.pallas_call(
        functools.partial(_ffn_kernel, sub=sub),
        grid=(batch, seq_len // ts),
        in_specs=[prev, cur, nxt, p_spec] + [_const_spec(w.shape) for w in weights],
        out_specs=cur,
        out_shape=jax.ShapeDtypeStruct(x.shape, x.dtype),
        scratch_shapes=[pltpu.VMEM(
            (ts // sub, 2 * D_FF // LANES, sub + 2 * HALO, LANES), jnp.float32)],
        compiler_params=_compiler_params(),
        name="ffn",
    )(x, x, x, p, *weights)


def kernel(x_prompt, x_sample, p_prompt, p_sample, g_mix, w_in, pool_w, pool_scale, g_v, sgu_w, sgu_b, w_gproj, w_o, g_ffn, w_up, conv_w, conv_b, w_down, g_pe, w_pe, w_pg, g_final):
    bf16 = jnp.bfloat16
    depth = w_in.shape[0]
    assert depth == 1

    def row(v):
        return v.reshape(1, -1)

    head_of = jnp.arange(D_SGU) // SGU_HEAD_DIM
    msmat = jnp.where(head_of[:, None] == head_of[None, :],
                      1.0 / SGU_HEAD_DIM, 0.0).astype(bf16)

    xs = [x_prompt, x_sample]
    ps = [p_prompt, p_sample]
    for i in range(depth):
        sguw = sgu_w[i].astype(bf16).reshape(N_SGU_HEADS // 2, 2 * CHUNK, CHUNK)
        sgub = jnp.repeat(sgu_b[i].T, SGU_HEAD_DIM, axis=1)
        mixer_w = (row(g_mix[i]), w_in[i, :, :D_POOL].astype(bf16),
                   w_in[i, :, D_POOL:].astype(bf16), pool_w[i].astype(bf16),
                   row(pool_scale[i]), row(g_v[i]), msmat, sguw, sgub,
                   w_gproj[i].astype(bf16), w_o[i].astype(bf16))
        ffn_w = (row(g_ffn[i]), w_up[i].astype(bf16), conv_w[i], row(conv_b[i]),
                 w_down[i].astype(bf16), row(g_pe[i]), w_pg[i].astype(bf16),
                 w_pe[i].astype(bf16), row(g_final))
        xs = [_ffn_call(_mixer_call(x, mixer_w), p[i], ffn_w)
              for x, p in zip(xs, ps)]
    return tuple(xs)
```
